```python
import math
import jax, jax.numpy as jnp
from jax import lax
import numpy as np


D_MODEL = 2048
BATCH = 8
SEQ = 4096
DEPTH = 1

SSM_WIDTH = D_MODEL // 2
SSM_GROUP = 16
SSM_N_GROUPS = SSM_WIDTH // SSM_GROUP
SSM_STATE = 64
ATTN_HEAD_DIM = 64
ATTN_Q_HEADS = (D_MODEL - SSM_WIDTH) // ATTN_HEAD_DIM
ATTN_KV_HEADS = 4
ATTN_REP = ATTN_Q_HEADS // ATTN_KV_HEADS
ATTN_Q_WIDTH = ATTN_Q_HEADS * ATTN_HEAD_DIM
ATTN_KV_WIDTH = ATTN_KV_HEADS * ATTN_HEAD_DIM
WINDOW = 128
MIX_WIDTH = SSM_WIDTH + ATTN_Q_WIDTH
IN_PROJ_WIDTH = SSM_WIDTH + ATTN_Q_WIDTH + 2 * ATTN_KV_WIDTH
REL_BUCKETS = 32
REL_MAX_DIST = 128
MEM_LEN = 256
MEM_HEADS = 4
MEM_HEAD_DIM = 128
MEM_WIDTH = MEM_HEADS * MEM_HEAD_DIM
PEER_HEADS = 8
PEER_N_KEYS = 128
PEER_N_EXPERTS = PEER_N_KEYS * PEER_N_KEYS
PEER_TOPK = 16
PEER_KEY_DIM = 256
PEER_HALF = PEER_KEY_DIM // 2
PEER_TOKEN_BLOCK = 128
NORM_EPS = 1e-6
DT_MIN = 1e-3
DT_MAX = 1e-1

kernel_name = 'hybrid_s5_swa_sink_peer_block'

F32 = jnp.float32


def rmsnorm(x, g):
    x32 = x.astype(F32)
    y = x32 * lax.rsqrt(jnp.mean(x32 * x32, axis=-1, keepdims=True) + NORM_EPS)
    return (y * g.astype(F32)).astype(x.dtype)


def s5_mixer(u, lam_re, lam_im, b_re, b_im, c_re, c_im, d, log_dt, w_glu, b_glu):
    bsz, L, _ = u.shape
    lam = lax.complex(lam_re.astype(F32), lam_im.astype(F32))
    dt = jnp.exp(log_dt.astype(F32))[:, None]
    lam_bar = jnp.exp(lam * dt)
    b = lax.complex(b_re.astype(F32), b_im.astype(F32))
    b_bar = ((lam_bar - 1.0) / lam)[..., None] * b
    c = lax.complex(c_re.astype(F32), c_im.astype(F32))
    ug = u.astype(F32).reshape(bsz, L, SSM_N_GROUPS, SSM_GROUP)

    def combine(left, right):
        a_l, s_l = left
        a_r, s_r = right
        return a_r * a_l, a_r * s_l + s_r

    def scan_one(u_seq):
        bu = jnp.einsum('gph,lgh->lgp', b_bar, u_seq.astype(jnp.complex64))
        a = jnp.broadcast_to(lam_bar, bu.shape)
        _, states = lax.associative_scan(combine, (a, bu), axis=0)
        return jnp.einsum('ghp,lgp->lgh', c, states).real

    y = lax.map(scan_one, ug) + d.astype(F32)[None, None] * ug
    y = jax.nn.gelu(y.reshape(bsz, L, SSM_WIDTH))
    return y * jax.nn.sigmoid(y @ w_glu.astype(F32) + b_glu.astype(F32))


def t5_bucket(dist):
    max_exact = REL_BUCKETS // 2
    d_f = jnp.maximum(dist, 1).astype(F32)
    large = max_exact + (jnp.log(d_f / max_exact) / math.log(REL_MAX_DIST / max_exact)
                         * (REL_BUCKETS - max_exact)).astype(jnp.int32)
    large = jnp.minimum(large, REL_BUCKETS - 1)
    return jnp.where(dist < max_exact, dist, large)


def sliding_window_gqa_sinks(q, k, v, sinks, rel_bias):
    bsz, L = q.shape[0], q.shape[1]
    nb = L // WINDOW
    qb = q.reshape(bsz, nb, WINDOW, ATTN_KV_HEADS, ATTN_REP, ATTN_HEAD_DIM)

    def windows(t):
        tp = jnp.pad(t, ((0, 0), (WINDOW, 0), (0, 0), (0, 0)))
        tp = tp.reshape(bsz, nb + 1, WINDOW, ATTN_KV_HEADS, ATTN_HEAD_DIM)
        return jnp.concatenate([tp[:, :-1], tp[:, 1:]], axis=2)

    kw = windows(k)
    vw = windows(v)
    scores = jnp.einsum('bnqgrd,bnkgd->bngrqk', qb, kw).astype(F32) * (ATTN_HEAD_DIM ** -0.5)
    qi = jnp.arange(WINDOW)[:, None]
    kj = jnp.arange(2 * WINDOW)[None, :]
    dist = qi + WINDOW - kj
    in_window = (dist >= 0) & (dist < WINDOW)
    key_pos = jnp.arange(nb)[:, None, None] * WINDOW + kj[None] - WINDOW
    valid = in_window[None] & (key_pos >= 0)
    bias = rel_bias.astype(F32)[t5_bucket(jnp.clip(dist, 0, WINDOW - 1))]
    bias = bias.transpose(2, 0, 1).reshape(ATTN_KV_HEADS, ATTN_REP, WINDOW, 2 * WINDOW)
    scores = jnp.where(valid[None, :, None, None], scores + bias[None, None], -jnp.inf)
    sink = jnp.broadcast_to(
        sinks.astype(F32).reshape(ATTN_KV_HEADS, ATTN_REP)[None, None, :, :, None, None],
        scores.shape[:-1] + (1,))
    p = jax.nn.softmax(jnp.concatenate([scores, sink], axis=-1), axis=-1)[..., :-1]
    out = jnp.einsum('bngrqk,bnkgd->bnqgrd', p.astype(vw.dtype), vw)
    return out.reshape(bsz, L, ATTN_Q_WIDTH)


def memory_cross_attention(hn, mem_n, w_cq, w_ckv, w_co):
    bsz, L, _ = hn.shape
    q = (hn @ w_cq).reshape(bsz, L, MEM_HEADS, MEM_HEAD_DIM)
    kv = (mem_n @ w_ckv).reshape(bsz, mem_n.shape[1], 2, MEM_HEADS, MEM_HEAD_DIM)
    k = kv[:, :, 0]
    v = kv[:, :, 1]
    s = jnp.einsum('blhd,bmhd->bhlm', q, k).astype(F32) * (MEM_HEAD_DIM ** -0.5)
    p = jax.nn.softmax(s, axis=-1)
    o = jnp.einsum('bhlm,bmhd->blhd', p.astype(v.dtype), v).reshape(bsz, L, MEM_WIDTH)
    return o @ w_co


def peer_ffn(hn, w_q, sub_keys, u_tab, v_tab):
    bsz, L, D = hn.shape
    tokens = hn.reshape(-1, PEER_TOKEN_BLOCK, D)

    def block(xb):
        T = xb.shape[0]
        q = (xb @ w_q).reshape(T, PEER_HEADS, 2, PEER_HALF)
        s = jnp.einsum('thcd,hckd->thck', q, sub_keys).astype(F32)
        s_top, i_top = lax.top_k(s, PEER_TOPK)
        cand = s_top[:, :, 0, :, None] + s_top[:, :, 1, None, :]
        cand_idx = i_top[:, :, 0, :, None] * PEER_N_KEYS + i_top[:, :, 1, None, :]
        best, pos = lax.top_k(cand.reshape(T, PEER_HEADS, PEER_TOPK * PEER_TOPK), PEER_TOPK)
        idx = jnp.take_along_axis(cand_idx.reshape(T, PEER_HEADS, PEER_TOPK * PEER_TOPK), pos, axis=-1)
        g = jax.nn.softmax(best, axis=-1)
        act = jax.nn.gelu(jnp.einsum('td,thkd->thk', xb, u_tab[idx]).astype(F32))
        coef = (g * act).astype(xb.dtype)
        return jnp.einsum('thk,thkd->td', coef, v_tab[idx])

    return lax.map(block, tokens).reshape(bsz, L, D)


def setup_inputs(seed: int = 0) -> dict:
    key = jax.random.key(seed)
    ks = jax.random.split(key, 32)
    nrm = lambda k, shape, s: jax.random.normal(k, shape, F32) * s
    G, P, H = SSM_N_GROUPS, SSM_STATE, SSM_GROUP
    x = jax.random.normal(ks[0], (BATCH, SEQ, D_MODEL), F32)
    mem = jax.random.normal(ks[1], (BATCH, MEM_LEN, D_MODEL), F32)
    rel_bias = nrm(ks[2], (REL_BUCKETS, ATTN_Q_HEADS), 0.5)
    norm_mix = 1.0 + nrm(ks[3], (DEPTH, D_MODEL), 0.02)
    w_in = nrm(ks[4], (DEPTH, D_MODEL, IN_PROJ_WIDTH), D_MODEL ** -0.5)
    n_idx = jnp.arange(P, dtype=F32)
    ssm_lambda_re = -0.5 + nrm(ks[5], (DEPTH, G, P), 0.01)
    ssm_lambda_im = math.pi * n_idx[None, None] + nrm(ks[6], (DEPTH, G, P), 0.01)
    ssm_b_re = nrm(ks[7], (DEPTH, G, P, H), (2 * H) ** -0.5)
    ssm_b_im = nrm(ks[8], (DEPTH, G, P, H), (2 * H) ** -0.5)
    ssm_c_re = nrm(ks[9], (DEPTH, G, H, P), (2 * P) ** -0.5)
    ssm_c_im = nrm(ks[10], (DEPTH, G, H, P), (2 * P) ** -0.5)
    ssm_d = nrm(ks[11], (DEPTH, G, H), 1.0)
    ssm_log_dt = jax.random.uniform(ks[12], (DEPTH, G), F32, math.log(DT_MIN), math.log(DT_MAX))
    ssm_w_glu = nrm(ks[13], (DEPTH, SSM_WIDTH, SSM_WIDTH), SSM_WIDTH ** -0.5)
    ssm_b_glu = nrm(ks[14], (DEPTH, SSM_WIDTH), 0.01)
    attn_sinks = nrm(ks[15], (DEPTH, ATTN_Q_HEADS), 1.0)
    w_out = nrm(ks[16], (DEPTH, MIX_WIDTH, D_MODEL), MIX_WIDTH ** -0.5)
    norm_cross = 1.0 + nrm(ks[17], (DEPTH, D_MODEL), 0.02)
    norm_mem = 1.0 + nrm(ks[18], (DEPTH, D_MODEL), 0.02)
    w_cq = nrm(ks[19], (DEPTH, D_MODEL, MEM_WIDTH), D_MODEL ** -0.5)
    w_ckv = nrm(ks[20], (DEPTH, D_MODEL, 2 * MEM_WIDTH), D_MODEL ** -0.5)
    w_co = nrm(ks[21], (DEPTH, MEM_WIDTH, D_MODEL), MEM_WIDTH ** -0.5)
    norm_ffn = 1.0 + nrm(ks[22], (DEPTH, D_MODEL), 0.02)
    peer_w_q = nrm(ks[23], (DEPTH, D_MODEL, PEER_HEADS * PEER_KEY_DIM), D_MODEL ** -0.5)
    peer_sub_keys = nrm(ks[24], (DEPTH, PEER_HEADS, 2, PEER_N_KEYS, PEER_HALF), PEER_HALF ** -0.5)
    peer_u = nrm(ks[25], (DEPTH, PEER_N_EXPERTS, D_MODEL), D_MODEL ** -0.5)
    peer_v = nrm(ks[26], (DEPTH, PEER_N_EXPERTS, D_MODEL), 0.1)
    norm_final = 1.0 + nrm(ks[27], (D_MODEL,), 0.02)
    return {'x': x, 'mem': mem, 'rel_bias': rel_bias, 'norm_mix': norm_mix, 'w_in': w_in,
            'ssm_lambda_re': ssm_lambda_re, 'ssm_lambda_im': ssm_lambda_im,
            'ssm_b_re': ssm_b_re, 'ssm_b_im': ssm_b_im, 'ssm_c_re': ssm_c_re, 'ssm_c_im': ssm_c_im,
            'ssm_d': ssm_d, 'ssm_log_dt': ssm_log_dt, 'ssm_w_glu': ssm_w_glu, 'ssm_b_glu': ssm_b_glu,
            'attn_sinks': attn_sinks, 'w_out': w_out, 'norm_cross': norm_cross, 'norm_mem': norm_mem,
            'w_cq': w_cq, 'w_ckv': w_ckv, 'w_co': w_co, 'norm_ffn': norm_ffn,
            'peer_w_q': peer_w_q, 'peer_sub_keys': peer_sub_keys, 'peer_u': peer_u, 'peer_v': peer_v,
            'norm_final': norm_final}


def reference(x, mem, rel_bias, norm_mix, w_in, ssm_lambda_re, ssm_lambda_im, ssm_b_re, ssm_b_im,
              ssm_c_re, ssm_c_im, ssm_d, ssm_log_dt, ssm_w_glu, ssm_b_glu, attn_sinks, w_out,
              norm_cross, norm_mem, w_cq, w_ckv, w_co, norm_ffn, peer_w_q, peer_sub_keys,
              peer_u, peer_v, norm_final):
    bsz, L, _ = x.shape
    h = x
    for l in range(DEPTH):
        proj = rmsnorm(h, norm_mix[l]) @ w_in[l]
        u_ssm = proj[..., :SSM_WIDTH]
        o = SSM_WIDTH
        q = proj[..., o:o + ATTN_Q_WIDTH].reshape(bsz, L, ATTN_Q_HEADS, ATTN_HEAD_DIM)
        o += ATTN_Q_WIDTH
        k = proj[..., o:o + ATTN_KV_WIDTH].reshape(bsz, L, ATTN_KV_HEADS, ATTN_HEAD_DIM)
        o += ATTN_KV_WIDTH
        v = proj[..., o:o + ATTN_KV_WIDTH].reshape(bsz, L, ATTN_KV_HEADS, ATTN_HEAD_DIM)
        y_ssm = s5_mixer(u_ssm, ssm_lambda_re[l], ssm_lambda_im[l], ssm_b_re[l], ssm_b_im[l],
                         ssm_c_re[l], ssm_c_im[l], ssm_d[l], ssm_log_dt[l], ssm_w_glu[l], ssm_b_glu[l])
        y_attn = sliding_window_gqa_sinks(q, k, v, attn_sinks[l], rel_bias)
        y_mix = jnp.concatenate([y_ssm.astype(h.dtype), y_attn.astype(h.dtype)], axis=-1)
        h = h + y_mix @ w_out[l]
        h = h + memory_cross_attention(rmsnorm(h, norm_cross[l]), rmsnorm(mem, norm_mem[l]),
                                       w_cq[l], w_ckv[l], w_co[l])
        h = h + peer_ffn(rmsnorm(h, norm_ffn[l]), peer_w_q[l], peer_sub_keys[l], peer_u[l], peer_v[l])
    return rmsnorm(h, norm_final)
```

```python
import functools
import math

import jax
import jax.numpy as jnp
from jax import lax
from jax.experimental import pallas as pl
from jax.experimental.pallas import tpu as pltpu

F32 = jnp.float32
BF16 = jnp.bfloat16
NORM_EPS = 1e-6
NEG = -1e30

VMEM_LIMIT_BYTES = 56 * 1024 * 1024
LANES = 128
SUBLANES = 8

SSM_GROUP = 16
SSM_STATE = 64
SSM_CHUNK = 16
ATTN_HEAD_DIM = 64
ATTN_KV_HEADS = 4
ATTN_REP = 4
WINDOW = 128
REL_BUCKETS = 32
REL_MAX_DIST = 128
MEM_HEADS = 4
MEM_HEAD_DIM = 128
PEER_HEADS = 8
PEER_N_KEYS = 128
PEER_TOPK = 16
PEER_HALF = 128


def _gelu_tanh(x):
    return 0.5 * x * (1.0 + jnp.tanh(math.sqrt(2.0 / math.pi) * (x + 0.044715 * (x * x * x))))


def _rmsnorm_f32(x, g):
    ms = jnp.mean(x * x, axis=-1, keepdims=True)
    return x * lax.rsqrt(ms + NORM_EPS) * g


def _params(*sem):
    return pltpu.CompilerParams(dimension_semantics=sem, vmem_limit_bytes=VMEM_LIMIT_BYTES)


def _const_spec(shape):
    nd = len(shape)
    return pl.BlockSpec(shape, lambda *_: (0,) * nd, pipeline_mode=pl.Buffered(1))


def _rms_matmul_kernel(x_ref, g_ref, w_ref, o_ref):
    xn = _rmsnorm_f32(x_ref[...].astype(F32), g_ref[...])
    o_ref[...] = jnp.dot(xn.astype(BF16), w_ref[...], preferred_element_type=F32).astype(o_ref.dtype)


def rms_matmul(x, gain, w, out_dtype, tm):
    n, d = x.shape
    m = w.shape[1]
    return pl.pallas_call(
        _rms_matmul_kernel,
        grid=(n // tm,),
        in_specs=[pl.BlockSpec((tm, d), lambda i: (i, 0)), _const_spec((1, d)), _const_spec((d, m))],
        out_specs=pl.BlockSpec((tm, m), lambda i: (i, 0)),
        out_shape=jax.ShapeDtypeStruct((n, m), out_dtype),
        compiler_params=_params("arbitrary"),
        name="rms_matmul",
    )(x, gain.reshape(1, d).astype(F32), w)


def _s5_states_kernel(u_ref, mo_ref, s_ref):
    u2 = jnp.concatenate([u_ref[0], u_ref[1]], axis=1)
    s_ref[...] = jnp.dot(u2, mo_ref[0], preferred_element_type=F32)


def s5_states(ug, m_out2):
    g, r, k = ug.shape
    return pl.pallas_call(
        _s5_states_kernel,
        grid=(g // 2,),
        in_specs=[pl.BlockSpec((2, r, k), lambda i: (i, 0, 0)),
                  pl.BlockSpec((1, 2 * k, 2 * LANES), lambda i: (i, 0, 0))],
        out_specs=pl.BlockSpec((r, 2 * LANES), lambda i: (0, i)),
        out_shape=jax.ShapeDtypeStruct((r, (g // 2) * 2 * LANES), F32),
        compiler_params=_params("arbitrary"),
        name="s5_states",
    )(ug, m_out2)


def _s5_scan_kernel(s_ref, ar_ref, ai_ref, xp_ref, *, n_chunks, rows_per_chunk, width):
    ar = jnp.broadcast_to(ar_ref[...], (rows_per_chunk, width // 2))
    ai = jnp.broadcast_to(ai_ref[...], (rows_per_chunk, width // 2))
    n_pairs = width // (2 * LANES)

    def body(c, carry):
        xr, xi = carry
        row = pl.multiple_of(c * rows_per_chunk, rows_per_chunk)
        for p in range(n_pairs):
            xp_ref[pl.ds(row, rows_per_chunk), (2 * p) * LANES:(2 * p + 1) * LANES] = xr[:, p * LANES:(p + 1) * LANES]
            xp_ref[pl.ds(row, rows_per_chunk), (2 * p + 1) * LANES:(2 * p + 2) * LANES] = xi[:, p * LANES:(p + 1) * LANES]
        s = s_ref[pl.ds(row, rows_per_chunk), :]
        sr = jnp.concatenate([s[:, (2 * p) * LANES:(2 * p + 1) * LANES] for p in range(n_pairs)], axis=1)
        si = jnp.concatenate([s[:, (2 * p + 1) * LANES:(2 * p + 2) * LANES] for p in range(n_pairs)], axis=1)
        nxr = ar * xr - ai * xi + sr
        nxi = ar * xi + ai * xr + si
        return nxr, nxi

    zero = jnp.zeros((rows_per_chunk, width // 2), F32)
    lax.fori_loop(0, n_chunks, body, (zero, zero))


def s5_scan(s, a_re, a_im, rows_per_chunk, width):
    r, total = s.shape
    n_chunks = r // rows_per_chunk
    kern = functools.partial(_s5_scan_kernel, n_chunks=n_chunks, rows_per_chunk=rows_per_chunk, width=width)
    return pl.pallas_call(
        kern,
        grid=(total // width,),
        in_specs=[pl.BlockSpec((r, width), lambda i: (0, i)),
                  pl.BlockSpec((1, width // 2), lambda i: (0, i)),
                  pl.BlockSpec((1, width // 2), lambda i: (0, i))],
        out_specs=pl.BlockSpec((r, width), lambda i: (0, i)),
        out_shape=jax.ShapeDtypeStruct((r, total), F32),
        compiler_params=_params("arbitrary"),
        name="s5_scan",
    )(s, a_re, a_im)


def _s5_out_kernel(u_ref, mi_ref, xp_ref, min_ref, y_ref):
    k = u_ref.shape[2]
    carry = jnp.dot(xp_ref[...].astype(BF16), min_ref[0], preferred_element_type=F32)
    for j in range(2):
        intra = jnp.dot(u_ref[j], mi_ref[j], preferred_element_type=F32)
        y_ref[j] = (intra + carry[:, j * k:(j + 1) * k]).astype(y_ref.dtype)


def s5_out(ug, m_intra, xprev, m_in2):
    g, r, k = ug.shape
    return pl.pallas_call(
        _s5_out_kernel,
        grid=(g // 2,),
        in_specs=[pl.BlockSpec((2, r, k), lambda i: (i, 0, 0)),
                  pl.BlockSpec((2, k, k), lambda i: (i, 0, 0)),
                  pl.BlockSpec((r, 2 * LANES), lambda i: (0, i)),
                  pl.BlockSpec((1, 2 * LANES, 2 * k), lambda i: (i, 0, 0))],
        out_specs=pl.BlockSpec((2, r, k), lambda i: (i, 0, 0)),
        out_shape=jax.ShapeDtypeStruct((g, r, k), BF16),
        compiler_params=_params("arbitrary"),
        name="s5_out",
    )(ug, m_intra, xprev, m_in2)


def _s5_weights(lam_re, lam_im, b_re, b_im, c_re, c_im, d, log_dt):
    t = SSM_CHUNK
    g, p = lam_re.shape
    h = SSM_GROUP
    lam = lax.complex(lam_re.astype(F32), lam_im.astype(F32))
    dt = jnp.exp(log_dt.astype(F32))[:, None]
    lam_bar = jnp.exp(lam * dt)
    b = lax.complex(b_re.astype(F32), b_im.astype(F32))
    b_bar = ((lam_bar - 1.0) / lam)[..., None] * b
    c = lax.complex(c_re.astype(F32), c_im.astype(F32))
    ks = jnp.arange(t + 1, dtype=F32)
    pw = jnp.exp((lam * dt)[None] * ks[:, None, None])
    kmat = jnp.einsum('ghp,kgp,gpj->gkhj', c, pw[:t], b_bar).real
    kmat = kmat.at[:, 0].add(jax.vmap(jnp.diag)(d.astype(F32)))
    tt = jnp.arange(t)
    lag = tt[None, :] - tt[:, None]
    kk = kmat[:, jnp.clip(lag, 0, t - 1)]
    kk = jnp.where((lag >= 0)[None, :, :, None, None], kk, 0.0)
    m_intra = kk.transpose(0, 1, 4, 2, 3).reshape(g, t * h, t * h)
    mo = pw[:t][::-1].transpose(1, 0, 2)[:, :, None, :] * b_bar.transpose(0, 2, 1)[:, None, :, :]
    mo = mo.reshape(g, t * h, p)
    mc = c[:, None, :, :] * pw[1:t + 1].transpose(1, 0, 2)[:, :, None, :]
    mc = mc.reshape(g, t * h, p).transpose(0, 2, 1)
    a = pw[t]

    k = t * h
    z = jnp.zeros((g // 2, k, p), F32)
    mo_e, mo_o = mo[0::2], mo[1::2]
    m_out2 = jnp.concatenate([
        jnp.concatenate([mo_e.real, z, mo_e.imag, z], axis=2),
        jnp.concatenate([z, mo_o.real, z, mo_o.imag], axis=2)], axis=1)
    zc = jnp.zeros((g // 2, p, k), F32)
    mc_e, mc_o = mc[0::2], mc[1::2]
    m_in2 = jnp.concatenate([
        jnp.concatenate([mc_e.real, zc], axis=2),
        jnp.concatenate([zc, mc_o.real], axis=2),
        jnp.concatenate([-mc_e.imag, zc], axis=2),
        jnp.concatenate([zc, -mc_o.imag], axis=2)], axis=1)
    a_re = a.real.reshape(1, g * p)
    a_im = a.imag.reshape(1, g * p)
    return m_intra.astype(BF16), m_out2.astype(BF16), m_in2.astype(BF16), a_re, a_im


def s5_mixer_pre_gelu(u_bld, weights):
    m_intra, m_out2, m_in2, a_re, a_im = weights
    bsz, seq, width = u_bld.shape
    t, h = SSM_CHUNK, SSM_GROUP
    g = width // h
    nc = seq // t
    ug = u_bld.reshape(bsz, nc, t, g, h).transpose(3, 1, 0, 2, 4).reshape(g, nc * bsz, t * h)
    s = s5_states(ug, m_out2)
    xprev = s5_scan(s, a_re, a_im, rows_per_chunk=bsz, width=min(8 * LANES, s.shape[1]))
    yg = s5_out(ug, m_intra, xprev, m_in2)
    return yg.reshape(g, nc, bsz, t, h).transpose(2, 1, 3, 0, 4).reshape(bsz, seq, width)


def _swa_kernel(sink_ref, q_ref, kp_ref, kc_ref, vp_ref, vc_ref, bp_ref, bc_ref, o_ref):
    n = pl.program_id(1)
    n_heads = ATTN_KV_HEADS * ATTN_REP
    lane = lax.broadcasted_iota(jnp.int32, (WINDOW, LANES), 1)
    lo = lane < ATTN_HEAD_DIM
    nt = (((1,), (1,)), ((), ()))
    outs = []
    for h in range(n_heads):
        g = h // ATTN_REP
        slab = h // 2
        qs = q_ref[0, :, slab * LANES:(slab + 1) * LANES]
        qm = jnp.where(lo if h % 2 == 0 else jnp.logical_not(lo), qs, jnp.zeros_like(qs))
        kp = kp_ref[0, :, g * LANES:(g + 1) * LANES]
        kc = kc_ref[0, :, g * LANES:(g + 1) * LANES]
        sp = lax.dot_general(qm, kp, nt, preferred_element_type=F32) + bp_ref[h]
        sp = jnp.where(n > 0, sp, NEG)
        sc = lax.dot_general(qm, kc, nt, preferred_element_type=F32) + bc_ref[h]
        sink = sink_ref[h]
        mx = jnp.maximum(jnp.maximum(jnp.max(sp, axis=1, keepdims=True), jnp.max(sc, axis=1, keepdims=True)), sink)
        pp = jnp.exp(sp - mx)
        pc = jnp.exp(sc - mx)
        den = jnp.sum(pp, axis=1, keepdims=True) + jnp.sum(pc, axis=1, keepdims=True) + jnp.exp(sink - mx)
        o = jnp.dot(pp.astype(BF16), vp_ref[0, :, g * LANES:(g + 1) * LANES], preferred_element_type=F32)
        o = o + jnp.dot(pc.astype(BF16), vc_ref[0, :, g * LANES:(g + 1) * LANES], preferred_element_type=F32)
        outs.append(o / den)
    for slab in range(n_heads // 2):
        o_ref[0, :, slab * LANES:(slab + 1) * LANES] = jnp.where(lo, outs[2 * slab], outs[2 * slab + 1]).astype(o_ref.dtype)


def swa_attention(proj, sinks, bias_prev, bias_cur, q_col, k_col, v_col):
    bsz, seq, _ = proj.shape
    nb = seq // WINDOW
    n_heads = ATTN_KV_HEADS * ATTN_REP
    qw = n_heads * ATTN_HEAD_DIM
    kw = ATTN_KV_HEADS * LANES
    prev = lambda b, n: jnp.maximum(n - 1, 0)
    return pl.pallas_call(
        _swa_kernel,
        grid=(bsz, nb),
        in_specs=[pl.BlockSpec(memory_space=pltpu.SMEM),
                  pl.BlockSpec((1, WINDOW, qw), lambda b, n: (b, n, q_col)),
                  pl.BlockSpec((1, WINDOW, kw), lambda b, n: (b, prev(b, n), k_col)),
                  pl.BlockSpec((1, WINDOW, kw), lambda b, n: (b, n, k_col)),
                  pl.BlockSpec((1, WINDOW, kw), lambda b, n: (b, prev(b, n), v_col)),
                  pl.BlockSpec((1, WINDOW, kw), lambda b, n: (b, n, v_col)),
                  _const_spec((n_heads, WINDOW, WINDOW)),
                  _const_spec((n_heads, WINDOW, WINDOW))],
        out_specs=pl.BlockSpec((1, WINDOW, qw), lambda b, n: (b, n, 0)),
        out_shape=jax.ShapeDtypeStruct((bsz, seq, qw), BF16),
        compiler_params=_params("arbitrary", "arbitrary"),
        name="swa_attention",
    )(sinks.astype(F32), proj, proj, proj, proj, proj, bias_prev, bias_cur)


def _swa_bias(rel_bias):
    qi = jnp.arange(WINDOW)[:, None]
    kj = jnp.arange(2 * WINDOW)[None, :]
    dist = qi + WINDOW - kj
    valid = (dist >= 0) & (dist < WINDOW)
    dc = jnp.clip(dist, 0, WINDOW - 1)
    max_exact = REL_BUCKETS // 2
    d_f = jnp.maximum(dc, 1).astype(F32)
    large = max_exact + (jnp.log(d_f / max_exact) / math.log(REL_MAX_DIST / max_exact)
                         * (REL_BUCKETS - max_exact)).astype(jnp.int32)
    large = jnp.minimum(large, REL_BUCKETS - 1)
    bucket = jnp.where(dc < max_exact, dc, large)
    bias = rel_bias.astype(F32)[bucket].transpose(2, 0, 1)
    bias = jnp.where(valid[None], bias, NEG)
    return bias[:, :, :WINDOW], bias[:, :, WINDOW:]


def _glu_out_kernel(ys_ref, ya_ref, x_ref, wg_ref, bg_ref, wo_ref, o_ref):
    half = ys_ref.shape[1]
    y = _gelu_tanh(ys_ref[...].astype(F32))
    z = jnp.dot(y.astype(BF16), wg_ref[...], preferred_element_type=F32) + bg_ref[...]
    y_ssm = y * jax.nn.sigmoid(z)
    acc = jnp.dot(y_ssm.astype(BF16), wo_ref[:half, :], preferred_element_type=F32)
    acc = acc + jnp.dot(ya_ref[...], wo_ref[half:, :], preferred_element_type=F32)
    o_ref[...] = x_ref[...] + acc


def glu_out(y_ssm_pre, y_attn, x, w_glu, b_glu, w_out, tm):
    n, d = x.shape
    half = y_ssm_pre.shape[1]
    return pl.pallas_call(
        _glu_out_kernel,
        grid=(n // tm,),
        in_specs=[pl.BlockSpec((tm, half), lambda i: (i, 0)),
                  pl.BlockSpec((tm, half), lambda i: (i, 0)),
                  pl.BlockSpec((tm, d), lambda i: (i, 0)),
                  _const_spec((half, half)), _const_spec((1, half)), _const_spec((2 * half, d))],
        out_specs=pl.BlockSpec((tm, d), lambda i: (i, 0)),
        out_shape=jax.ShapeDtypeStruct((n, d), F32),
        compiler_params=_params("arbitrary"),
        name="glu_out",
    )(y_ssm_pre, y_attn, x, w_glu, b_glu.reshape(1, half).astype(F32), w_out)


def _cross_kernel(h_ref, kv_ref, g_ref, wq_ref, wo_ref, o_ref):
    hx = h_ref[0]
    hn = _rmsnorm_f32(hx, g_ref[...])
    q = jnp.dot(hn.astype(BF16), wq_ref[...], preferred_element_type=F32).astype(BF16)
    width = MEM_HEADS * MEM_HEAD_DIM
    nt = (((1,), (1,)), ((), ()))
    outs = []
    for h in range(MEM_HEADS):
        sl = slice(h * MEM_HEAD_DIM, (h + 1) * MEM_HEAD_DIM)
        k = kv_ref[0, :, sl]
        v = kv_ref[0, :, width + h * MEM_HEAD_DIM: width + (h + 1) * MEM_HEAD_DIM]
        s = lax.dot_general(q[:, sl], k, nt, preferred_element_type=F32) * (MEM_HEAD_DIM ** -0.5)
        mx = jnp.max(s, axis=1, keepdims=True)
        p = jnp.exp(s - mx)
        den = jnp.sum(p, axis=1, keepdims=True)
        outs.append(jnp.dot(p.astype(BF16), v, preferred_element_type=F32) / den)
    o = jnp.concatenate(outs, axis=1)
    o_ref[0] = hx + jnp.dot(o.astype(BF16), wo_ref[...], preferred_element_type=F32)


def cross_attention(h, kv, gain, w_cq, w_co, tq):
    bsz, seq, d = h.shape
    m_len, kvw = kv.shape[1], kv.shape[2]
    width = w_cq.shape[1]
    return pl.pallas_call(
        _cross_kernel,
        grid=(bsz, seq // tq),
        in_specs=[pl.BlockSpec((1, tq, d), lambda b, i: (b, i, 0)),
                  pl.BlockSpec((1, m_len, kvw), lambda b, i: (b, 0, 0)),
                  _const_spec((1, d)), _const_spec((d, width)), _const_spec((width, d))],
        out_specs=pl.BlockSpec((1, tq, d), lambda b, i: (b, i, 0)),
        out_shape=jax.ShapeDtypeStruct((bsz, seq, d), F32),
        compiler_params=_params("arbitrary", "arbitrary"),
        name="cross_attention",
    )(h, kv, gain.reshape(1, d).astype(F32), w_cq, w_co)


def _peer_topk_kernel(h_ref, g_ref, wq_ref, sk_ref, idx_ref, gate_ref, hn_scr):
    hd = pl.program_id(1)
    k_top = PEER_TOPK
    tb = h_ref.shape[0]

    @pl.when(hd == 0)
    def _():
        hn_scr[...] = _rmsnorm_f32(h_ref[...], g_ref[...]).astype(BF16)
        idx_ref[...] = jnp.zeros_like(idx_ref)
        gate_ref[...] = jnp.zeros_like(gate_ref)

    q = jnp.dot(hn_scr[...], wq_ref[...], preferred_element_type=F32).astype(BF16)
    nt = (((1,), (1,)), ((), ()))
    lane_f = lax.broadcasted_iota(jnp.int32, (tb, LANES), 1).astype(F32)
    pos = lax.broadcasted_iota(jnp.int32, (tb, k_top * k_top), 1)
    shift = k_top.bit_length() - 1
    pos_hi = pos >> shift
    pos_lo = pos & (k_top - 1)
    cand = jnp.zeros((tb, k_top * k_top), F32)
    cidx = jnp.zeros((tb, k_top * k_top), F32)
    for c in range(2):
        s = lax.dot_general(q[:, c * PEER_HALF:(c + 1) * PEER_HALF], sk_ref[0, c], nt, preferred_element_type=F32)
        sel_pos = pos_hi if c == 0 else pos_lo
        scale = float(PEER_N_KEYS) if c == 0 else 1.0
        for k in range(k_top):
            m = jnp.max(s, axis=1, keepdims=True)
            ix = jnp.min(jnp.where(s == m, lane_f, float(PEER_N_KEYS)), axis=1, keepdims=True)
            s = jnp.where(lane_f == ix, -jnp.inf, s)
            hit = sel_pos == k
            cand = jnp.where(hit, cand + m, cand)
            cidx = jnp.where(hit, cidx + ix * scale, cidx)
    n_exp = float(PEER_N_KEYS * PEER_N_KEYS)
    key = pos.astype(F32) * n_exp + cidx
    big = float(k_top * k_top) * n_exp
    out_lane = lax.broadcasted_iota(jnp.int32, (tb, LANES), 1)
    idx_acc = jnp.zeros((tb, LANES), F32)
    e_acc = jnp.zeros((tb, LANES), F32)
    den = jnp.zeros((tb, 1), F32)
    m0 = None
    for k in range(k_top):
        m = jnp.max(cand, axis=1, keepdims=True)
        kk = jnp.min(jnp.where(cand == m, key, big), axis=1, keepdims=True)
        cand = jnp.where(key == kk, -jnp.inf, cand)
        e_idx = kk - jnp.floor(kk * (1.0 / n_exp)) * n_exp
        if k == 0:
            m0 = m
        e = jnp.exp(m - m0)
        den = den + e
        hit = out_lane == hd * k_top + k
        idx_acc = jnp.where(hit, e_idx, idx_acc)
        e_acc = jnp.where(hit, e, e_acc)
    mine = (out_lane >> shift) == hd
    idx_ref[...] = jnp.where(mine, idx_acc.astype(jnp.int32), idx_ref[...])
    gate_ref[...] = jnp.where(mine, e_acc / den, gate_ref[...])


def peer_topk(h, gain, w_q, sub_keys, tb):
    n, d = h.shape
    heads = sub_keys.shape[0]
    hw = 2 * PEER_HALF
    return pl.pallas_call(
        _peer_topk_kernel,
        grid=(n // tb, heads),
        in_specs=[pl.BlockSpec((tb, d), lambda i, hd: (i, 0)),
                  _const_spec((1, d)),
                  pl.BlockSpec((d, hw), lambda i, hd: (0, hd)),
                  pl.BlockSpec((1, 2, PEER_N_KEYS, PEER_HALF), lambda i, hd: (hd, 0, 0, 0))],
        out_specs=[pl.BlockSpec((tb, LANES), lambda i, hd: (i, 0)),
                   pl.BlockSpec((tb, LANES), lambda i, hd: (i, 0))],
        out_shape=[jax.ShapeDtypeStruct((n, LANES), jnp.int32),
                   jax.ShapeDtypeStruct((n, LANES), F32)],
        scratch_shapes=[pltpu.VMEM((tb, d), BF16)],
        compiler_params=_params("arbitrary", "arbitrary"),
        name="peer_topk",
    )(h, gain.reshape(1, d).astype(F32), w_q, sub_keys)


PEER_GROUP = 8
PEER_SLOTS = 2


def _peer_gather_kernel(idx_ref, h_ref, gate_ref, gf_ref, gn_ref, tbl_ref, o_ref, buf, hn_scr, sem, *, final_norm):
    tb = h_ref.shape[0]
    n_sel = idx_ref.shape[1]
    n_groups = tb // PEER_GROUP
    d = h_ref.shape[1]

    hn_scr[...] = _rmsnorm_f32(h_ref[...], gf_ref[...]).astype(BF16).astype(F32)

    def start_group(grp, slot):
        def tok_body(t, carry):
            tok = grp * PEER_GROUP + t
            for j in range(n_sel):
                e = idx_ref[tok, j]
                pltpu.make_async_copy(tbl_ref.at[pl.ds(e, 1), :],
                                      buf.at[slot, pl.ds(t * n_sel + j, 1), :],
                                      sem.at[slot]).start()
            return carry
        lax.fori_loop(0, PEER_GROUP, tok_body, 0)

    def wait_group(slot):
        pltpu.make_async_copy(tbl_ref.at[pl.ds(0, PEER_GROUP * n_sel), :], buf.at[slot], sem.at[slot]).wait()

    start_group(0, 0)

    ones8 = jnp.ones((SUBLANES, LANES), F32)
    row8 = lax.broadcasted_iota(jnp.int32, (SUBLANES, LANES), 0)
    nt = (((1,), (1,)), ((), ()))

    def group_body(grp, carry):
        slot = grp % PEER_SLOTS

        @pl.when(grp + 1 < n_groups)
        def _():
            start_group(grp + 1, 1 - slot)

        wait_group(slot)
        base = pl.multiple_of(grp * PEER_GROUP, PEER_GROUP)
        x8 = hn_scr[pl.ds(base, PEER_GROUP), :]
        g8 = gate_ref[pl.ds(base, PEER_GROUP), :]
        act8 = jnp.zeros((PEER_GROUP, n_sel), F32)
        for t in range(PEER_GROUP):
            w = buf[slot, t * n_sel:(t + 1) * n_sel, :]
            u = lax.bitcast_convert_type(w & jnp.uint32(0xFFFF0000), F32)
            p = u * x8[t:t + 1, :]
            r = p[:, 0:LANES]
            for c in range(1, d // LANES):
                r = r + p[:, c * LANES:(c + 1) * LANES]
            a = lax.dot_general(ones8, r, nt, preferred_element_type=F32, precision=lax.Precision.HIGHEST)
            act8 = jnp.where(row8 == t, a, act8)
        coef8 = g8 * _gelu_tanh(act8)
        c_hi = coef8.astype(BF16)
        c_lo = (coef8 - c_hi.astype(F32)).astype(BF16)
        outs = []
        for t in range(PEER_GROUP):
            w = buf[slot, t * n_sel:(t + 1) * n_sel, :]
            v = lax.bitcast_convert_type(w << 16, F32).astype(BF16)
            lhs = jnp.concatenate([c_hi[t:t + 1, :], c_lo[t:t + 1, :]], axis=0)
            o2 = jnp.dot(lhs, v, preferred_element_type=F32)
            outs.append(o2[0:1, :] + o2[1:2, :])
        peer = jnp.concatenate(outs, axis=0)
        h3 = h_ref[pl.ds(base, PEER_GROUP), :] + peer
        o_ref[pl.ds(base, PEER_GROUP), :] = _rmsnorm_f32(h3, gn_ref[...]) if final_norm else h3
        return carry

    lax.fori_loop(0, n_groups, group_body, 0)


def peer_gather(h, idx, gate, gain_ffn, gain_final, table, tb, final_norm):
    n, d = h.shape
    n_sel = idx.shape[1]
    return pl.pallas_call(
        functools.partial(_peer_gather_kernel, final_norm=final_norm),
        grid=(n // tb,),
        in_specs=[pl.BlockSpec((tb, n_sel), lambda i: (i, 0), memory_space=pltpu.SMEM),
                  pl.BlockSpec((tb, d), lambda i: (i, 0)),
                  pl.BlockSpec((tb, n_sel), lambda i: (i, 0)),
                  _const_spec((1, d)), _const_spec((1, d)),
                  pl.BlockSpec(memory_space=pl.ANY)],
        out_specs=pl.BlockSpec((tb, d), lambda i: (i, 0)),
        out_shape=jax.ShapeDtypeStruct((n, d), F32),
        scratch_shapes=[pltpu.VMEM((PEER_SLOTS, PEER_GROUP * n_sel, d), jnp.uint32),
                        pltpu.VMEM((tb, d), F32),
                        pltpu.SemaphoreType.DMA((PEER_SLOTS,))],
        compiler_params=_params("arbitrary"),
        name="peer_gather",
    )(idx, h, gate, gain_ffn.reshape(1, d).astype(F32), gain_final.reshape(1, d).astype(F32), table)


def _pack_tables(u_tab, v_tab):
    ub = lax.bitcast_convert_type(u_tab.astype(BF16), jnp.uint16).astype(jnp.uint32)
    vb = lax.bitcast_convert_type(v_tab.astype(BF16), jnp.uint16).astype(jnp.uint32)
    return (ub << 16) | vb


def kernel(x, mem, rel_bias, norm_mix, w_in, ssm_lambda_re, ssm_lambda_im, ssm_b_re, ssm_b_im, ssm_c_re, ssm_c_im, ssm_d, ssm_log_dt, ssm_w_glu, ssm_b_glu, attn_sinks, w_out, norm_cross, norm_mem, w_cq, w_ckv, w_co, norm_ffn, peer_w_q, peer_sub_keys, peer_u, peer_v, norm_final):
    bsz, seq, d = x.shape
    depth = w_in.shape[0]
    n = bsz * seq
    ssm_w = ssm_w_glu.shape[1]
    q_w = ATTN_KV_HEADS * ATTN_REP * ATTN_HEAD_DIM
    kv_w = ATTN_KV_HEADS * ATTN_HEAD_DIM
    bias_prev, bias_cur = _swa_bias(rel_bias)

    def dup_heads(w):
        w4 = w.reshape(d, ATTN_KV_HEADS, 1, ATTN_HEAD_DIM)
        return jnp.broadcast_to(w4, (d, ATTN_KV_HEADS, 2, ATTN_HEAD_DIM)).reshape(d, 2 * kv_w)

    h = x.reshape(n, d)
    for l in range(depth):
        wi = w_in[l]
        w_cat = jnp.concatenate([
            wi[:, :ssm_w],
            wi[:, ssm_w:ssm_w + q_w] * (ATTN_HEAD_DIM ** -0.5),
            dup_heads(wi[:, ssm_w + q_w:ssm_w + q_w + kv_w]),
            dup_heads(wi[:, ssm_w + q_w + kv_w:])], axis=1).astype(BF16)
        proj = rms_matmul(h, norm_mix[l], w_cat, BF16, tm=1024).reshape(bsz, seq, -1)

        s5w = _s5_weights(ssm_lambda_re[l], ssm_lambda_im[l], ssm_b_re[l], ssm_b_im[l],
                          ssm_c_re[l], ssm_c_im[l], ssm_d[l], ssm_log_dt[l])
        y_ssm_pre = s5_mixer_pre_gelu(proj[:, :, :ssm_w], s5w)
        y_attn = swa_attention(proj, attn_sinks[l], bias_prev, bias_cur,
                               q_col=ssm_w // q_w, k_col=(ssm_w + q_w) // (2 * kv_w),
                               v_col=(ssm_w + q_w + 2 * kv_w) // (2 * kv_w))
        h = glu_out(y_ssm_pre.reshape(n, ssm_w), y_attn.reshape(n, q_w), h,
                    ssm_w_glu[l].astype(BF16), ssm_b_glu[l], w_out[l].astype(BF16), tm=512)

        kv = rms_matmul(mem.reshape(-1, d), norm_mem[l], w_ckv[l].astype(BF16), BF16, tm=512)
        kv = kv.reshape(bsz, mem.shape[1], -1)
        h = cross_attention(h.reshape(bsz, seq, d), kv, norm_cross[l], w_cq[l].astype(BF16),
                            w_co[l].astype(BF16), tq=512).reshape(n, d)

        idx, gate = peer_topk(h, norm_ffn[l], peer_w_q[l].astype(BF16), peer_sub_keys[l].astype(BF16), tb=256)
        table = _pack_tables(peer_u[l], peer_v[l])
        h = peer_gather(h, idx, gate, norm_ffn[l], norm_final, table, tb=64, final_norm=(l == depth - 1))
    return h.reshape(bsz, seq, d)
```
